```python
import jax, jax.numpy as jnp
from jax import lax
import numpy as np

D_MODEL = 1024
BATCH = 8
SEQ = 2048
DEPTH = 4
DEC_BATCH = 128
DEC_SEQ = 8
PAST_LEN = 16384
PAGE_SIZE = 128

D_MIX = 2 * D_MODEL
W_POOL = D_MIX // 2
W_LRU = D_MIX - W_POOL
POOL_WINDOWS = (2, 4, 8, 16)
N_POOL_GROUPS = 4
POOL_GROUP = W_POOL // N_POOL_GROUPS
POOL_BUF = 15
LRU_HEADS = 8
LRU_HEAD_DIM = W_LRU // LRU_HEADS
CONV_WIDTH = 4
CONV_BUF = CONV_WIDTH - 1
LRU_C = 8.0
NORM_EPS = 1e-6
D_IN = 2 * W_POOL + 2 * W_LRU

kernel_name = "hymba_pool_rglru_decoder_step"


def rms_norm(x, g):
    xf = x.astype(jnp.float32)
    y = xf * lax.rsqrt(jnp.mean(xf * xf, axis=-1, keepdims=True) + NORM_EPS)
    return (y * g.astype(jnp.float32)).astype(x.dtype)


def pool_mixer(v, buf, start_pos, pool_w, pool_scale):
    B, T = v.shape[0], v.shape[1]
    ext = jnp.concatenate([buf.astype(jnp.float32), v.astype(jnp.float32)], axis=1)
    cs0 = jnp.concatenate([jnp.zeros((B, 1, W_POOL), jnp.float32), jnp.cumsum(ext, axis=1)], axis=1)
    pos = start_pos + jnp.arange(T, dtype=jnp.int32)
    xcur = ext[:, POOL_BUF:, :]
    outs = []
    for g in range(N_POOL_GROUPS):
        w = POOL_WINDOWS[g]
        c0, c1 = g * POOL_GROUP, (g + 1) * POOL_GROUP
        hi = cs0[:, POOL_BUF + 1:POOL_BUF + 1 + T, c0:c1]
        lo = cs0[:, POOL_BUF + 1 - w:POOL_BUF + 1 - w + T, c0:c1]
        cnt = jnp.minimum(pos + 1, w).astype(jnp.float32)[None, :, None]
        outs.append((hi - lo) / cnt - xcur[:, :, c0:c1])
    d = jnp.stack(outs, axis=2)
    y = jnp.einsum('btgc,gcd->btgd', d, pool_w.astype(jnp.float32)).reshape(B, T, W_POOL)
    y = y * pool_scale.astype(jnp.float32)[None, None, :]
    new_buf = ext[:, T:, :]
    return y.astype(v.dtype), new_buf.astype(buf.dtype)


def causal_conv(u, buf, conv_w, conv_b):
    T = u.shape[1]
    ext = jnp.concatenate([buf.astype(jnp.float32), u.astype(jnp.float32)], axis=1)
    cw = conv_w.astype(jnp.float32)
    y = conv_b.astype(jnp.float32)[None, None, :]
    for k in range(CONV_WIDTH):
        y = y + ext[:, k:k + T, :] * cw[k][None, None, :]
    return y.astype(u.dtype), ext[:, T:, :].astype(buf.dtype)


def rg_lru(xs, h0, start_pos, wa, ba, wx, bx, lam):
    B, T = xs.shape[0], xs.shape[1]
    xf = xs.astype(jnp.float32)
    xh = xf.reshape(B, T, LRU_HEADS, LRU_HEAD_DIM)
    r = jax.nn.sigmoid(jnp.einsum('bthi,hij->bthj', xh, wa.astype(jnp.float32)).reshape(B, T, W_LRU)
                       + ba.astype(jnp.float32)[None, None, :])
    gi = jax.nn.sigmoid(jnp.einsum('bthi,hij->bthj', xh, wx.astype(jnp.float32)).reshape(B, T, W_LRU)
                        + bx.astype(jnp.float32)[None, None, :])
    log_a = -LRU_C * r * jax.nn.softplus(-lam.astype(jnp.float32))[None, None, :]
    a = jnp.exp(log_a)
    mult = jnp.sqrt(jnp.maximum(-jnp.expm1(2.0 * log_a), 1e-12))
    pos = start_pos + jnp.arange(T, dtype=jnp.int32)
    is_start = (pos == 0).astype(jnp.float32)[None, :, None]
    mult = is_start + (1.0 - is_start) * mult
    u = xf * gi * mult

    def step(h, au):
        a_t, u_t = au
        h_new = a_t * h + u_t
        return h_new, h_new

    hT, hs = lax.scan(step, h0.astype(jnp.float32), (jnp.swapaxes(a, 0, 1), jnp.swapaxes(u, 0, 1)))
    return jnp.swapaxes(hs, 0, 1).astype(xs.dtype), hT.astype(h0.dtype)


def layer(x, pool_buf, conv_buf, h0, start_pos, norm_pre, norm_post, w_in, pool_w, pool_scale,
          conv_w, conv_b, lru_wa, lru_ba, lru_wx, lru_bx, lru_lam, w_out):
    xn = rms_norm(x, norm_pre)
    z = jnp.einsum('btd,de->bte', xn, w_in.astype(xn.dtype))
    v_pool = z[:, :, 0:W_POOL]
    g_pool = z[:, :, W_POOL:2 * W_POOL]
    u_lru = z[:, :, 2 * W_POOL:2 * W_POOL + W_LRU]
    g_lru = z[:, :, 2 * W_POOL + W_LRU:D_IN]
    y_pool, new_pool = pool_mixer(v_pool, pool_buf, start_pos, pool_w, pool_scale)
    xs, new_conv = causal_conv(u_lru, conv_buf, conv_w, conv_b)
    y_lru, new_h = rg_lru(xs, h0, start_pos, lru_wa, lru_ba, lru_wx, lru_bx, lru_lam)
    mixed = jnp.concatenate([y_pool * jax.nn.silu(g_pool), y_lru * jax.nn.silu(g_lru)], axis=-1)
    out = jnp.einsum('bte,ed->btd', mixed, w_out.astype(mixed.dtype))
    return x + rms_norm(out, norm_post), new_pool, new_conv, new_h


def setup_inputs(seed: int = 0) -> dict:
    key = jax.random.key(seed)
    ks = jax.random.split(key, 20)
    f32 = jnp.float32
    a0 = jax.random.uniform(ks[15], (DEPTH, W_LRU), f32, 0.9, 0.999)
    a_base = a0 ** (1.0 / LRU_C)
    return {
        "x_prompt": jax.random.normal(ks[0], (BATCH, SEQ, D_MODEL), f32),
        "x_sample": jax.random.normal(ks[1], (DEC_BATCH, DEC_SEQ, D_MODEL), f32),
        "state_pool": jax.random.normal(ks[2], (DEPTH, DEC_BATCH, POOL_BUF, W_POOL), f32),
        "state_conv": jax.random.normal(ks[3], (DEPTH, DEC_BATCH, CONV_BUF, W_LRU), f32),
        "state_lru": 0.5 * jax.random.normal(ks[4], (DEPTH, DEC_BATCH, W_LRU), f32),
        "norm_pre": 1.0 + 0.1 * jax.random.normal(ks[5], (DEPTH, D_MODEL), f32),
        "norm_post": 1.0 + 0.1 * jax.random.normal(ks[6], (DEPTH, D_MODEL), f32),
        "w_in": jax.random.normal(ks[7], (DEPTH, D_MODEL, D_IN), f32) * D_MODEL ** -0.5,
        "pool_w": jax.random.normal(ks[8], (DEPTH, N_POOL_GROUPS, POOL_GROUP, POOL_GROUP), f32) * POOL_GROUP ** -0.5,
        "pool_scale": 1.0 + 0.1 * jax.random.normal(ks[9], (DEPTH, W_POOL), f32),
        "conv_w": jax.random.normal(ks[10], (DEPTH, CONV_WIDTH, W_LRU), f32) * CONV_WIDTH ** -0.5,
        "conv_b": 0.02 * jax.random.normal(ks[11], (DEPTH, W_LRU), f32),
        "lru_wa": jax.random.normal(ks[12], (DEPTH, LRU_HEADS, LRU_HEAD_DIM, LRU_HEAD_DIM), f32) * LRU_HEAD_DIM ** -0.5,
        "lru_ba": 0.02 * jax.random.normal(ks[13], (DEPTH, W_LRU), f32),
        "lru_wx": jax.random.normal(ks[14], (DEPTH, LRU_HEADS, LRU_HEAD_DIM, LRU_HEAD_DIM), f32) * LRU_HEAD_DIM ** -0.5,
        "lru_bx": 0.02 * jax.random.normal(ks[16], (DEPTH, W_LRU), f32),
        "lru_lam": jnp.log(a_base) - jnp.log1p(-a_base),
        "w_out": jax.random.normal(ks[17], (DEPTH, D_MIX, D_MODEL), f32) * D_MIX ** -0.5,
    }


def reference(x_prompt, x_sample, state_pool, state_conv, state_lru, norm_pre, norm_post, w_in,
              pool_w, pool_scale, conv_w, conv_b, lru_wa, lru_ba, lru_wx, lru_bx, lru_lam, w_out):
    n_prompt = x_prompt.shape[0]
    xp = x_prompt
    xq = x_sample
    pp, cp, hp, psm, csm, hsm = [], [], [], [], [], []
    for l in range(DEPTH):
        params = (norm_pre[l], norm_post[l], w_in[l], pool_w[l], pool_scale[l], conv_w[l], conv_b[l],
                  lru_wa[l], lru_ba[l], lru_wx[l], lru_bx[l], lru_lam[l], w_out[l])
        zero_pool = jnp.zeros((n_prompt, POOL_BUF, W_POOL), state_pool.dtype)
        zero_conv = jnp.zeros((n_prompt, CONV_BUF, W_LRU), state_conv.dtype)
        zero_h = jnp.zeros((n_prompt, W_LRU), state_lru.dtype)
        xp, pool_p, conv_p, h_p = layer(xp, zero_pool, zero_conv, zero_h, 0, *params)
        pp.append(pool_p)
        cp.append(conv_p)
        hp.append(h_p)
        xq, pool_s, conv_s, h_s = layer(xq, state_pool[l], state_conv[l], state_lru[l], PAST_LEN, *params)
        psm.append(pool_s)
        csm.append(conv_s)
        hsm.append(h_s)
    new_pool_prompt = jnp.stack(pp, axis=0)
    new_conv_prompt = jnp.stack(cp, axis=0)
    new_lru_prompt = jnp.stack(hp, axis=0)
    new_pool_sample = jnp.stack(psm, axis=0)
    new_conv_sample = jnp.stack(csm, axis=0)
    new_lru_sample = jnp.stack(hsm, axis=0)
    return (xp, xq, new_pool_prompt, new_conv_prompt, new_lru_prompt, new_pool_sample, new_conv_sample, new_lru_sample)
```

```python
import functools

import jax
import jax.numpy as jnp
from jax import lax
from jax.experimental import pallas as pl
from jax.experimental.pallas import tpu as pltpu

D_MODEL = 1024
DEPTH = 4
W_POOL = 1024
W_LRU = 1024
D_MIX = W_POOL + W_LRU
D_IN = 2 * W_POOL + 2 * W_LRU
POOL_WINDOWS = (2, 4, 8, 16)
POOL_GROUP = W_POOL // len(POOL_WINDOWS)
POOL_BUF = 15
LRU_HEADS = 8
LRU_HEAD_DIM = W_LRU // LRU_HEADS
CONV_WIDTH = 4
CONV_BUF = CONV_WIDTH - 1
LRU_C = 8.0
NORM_EPS = 1e-6
PAST_LEN = 16384

LANES = 128
NORM_ROWS = 32
MIX_ROWS = 64
V7X_VMEM_BYTES = 64 * 1024 * 1024

F32 = jnp.float32
BF16 = jnp.bfloat16


def _dot(a, b):
    return jnp.dot(a, b, preferred_element_type=F32)


def _silu(g):
    h = 0.5 * g
    return h + h * jnp.tanh(h)


def _layer_body(x_ref, pool0_ref, conv0_ref, h0_ref, gpre_ref, gpost_ref, win_ref, poolw_ref,
                pscale_ref, cw_ref, cb_ref, wax_ref, ba_ref, bx_ref, lam_ref, wout_ref,
                y_ref, npool_ref, nconv_ref, nh_ref,
                vext, uext, h_s, act_bf, xs_bf, gp_s, gl_s, xs_s, ra_s, rx_s, yp_s, mixed_s,
                *, B, Tt, start_pos):
    i = pl.program_id(1)
    n_t = pl.num_programs(1)
    M = Tt * B
    PH = POOL_BUF * B
    CH = CONV_BUF * B

    @pl.when(i == 0)
    def _load_state():
        vext[0:PH, :] = pool0_ref[...]
        uext[0:CH, :] = conv0_ref[...]
        h_s[...] = h0_ref[...]

    def pre_norm(c, carry):
        r0 = pl.multiple_of(c * NORM_ROWS, NORM_ROWS)
        x = x_ref[pl.ds(r0, NORM_ROWS), :]
        ms = jnp.mean(x * x, axis=-1, keepdims=True)
        xn = x * lax.rsqrt(ms + NORM_EPS) * gpre_ref[...]
        act_bf[pl.ds(r0, NORM_ROWS), :] = xn.astype(BF16)
        return carry

    lax.fori_loop(0, M // NORM_ROWS, pre_norm, 0)

    xn = act_bf[...]
    vext[PH:PH + M, :] = _dot(xn, win_ref[:, 0:W_POOL])
    gp_s[...] = _dot(xn, win_ref[:, W_POOL:2 * W_POOL])
    uext[CH:CH + M, :] = _dot(xn, win_ref[:, 2 * W_POOL:2 * W_POOL + W_LRU])
    gl_s[...] = _dot(xn, win_ref[:, 2 * W_POOL + W_LRU:D_IN])

    def pool_conv(c, carry):
        r0 = pl.multiple_of(c * MIX_ROWS, MIX_ROWS)
        for g, w in enumerate(POOL_WINDOWS):
            cols = pl.ds(g * POOL_GROUP, POOL_GROUP)
            cur = vext[pl.ds(PH + r0, MIX_ROWS), cols]
            s = cur
            for j in range(1, w):
                s = s + vext[pl.ds(pl.multiple_of(PH - j * B + r0, 8), MIX_ROWS), cols]
            d = s * (1.0 / w) - cur
            act_bf[pl.ds(r0, MIX_ROWS), cols] = d.astype(BF16)
        for cc in range(W_LRU // POOL_GROUP):
            cols = pl.ds(cc * POOL_GROUP, POOL_GROUP)
            acc = cb_ref[:, cols]
            for k in range(CONV_WIDTH):
                acc = acc + uext[pl.ds(pl.multiple_of(k * B + r0, 8), MIX_ROWS), cols] * cw_ref[k:k + 1, cols]
            xs_s[pl.ds(r0, MIX_ROWS), cols] = acc
            xs_bf[pl.ds(r0, MIX_ROWS), cols] = acc.astype(BF16)
        return carry

    lax.fori_loop(0, M // MIX_ROWS, pool_conv, 0)

    if start_pos < POOL_BUF:
        fix_rows = (POOL_BUF + 1) * B
        assert fix_rows <= M

        @pl.when(i == 0)
        def _fix_counts():
            step = lax.broadcasted_iota(jnp.int32, (fix_rows, POOL_GROUP), 0) // B
            for g, w in enumerate(POOL_WINDOWS):
                cols = pl.ds(g * POOL_GROUP, POOL_GROUP)
                cur = vext[PH:PH + fix_rows, cols]
                s = cur
                for j in range(1, w):
                    s = s + vext[PH - j * B:PH - j * B + fix_rows, cols]
                cnt = jnp.minimum(step + (start_pos + 1), w).astype(F32)
                d = s / cnt - cur
                act_bf[0:fix_rows, cols] = d.astype(BF16)

    for g in range(len(POOL_WINDOWS)):
        cols = slice(g * POOL_GROUP, (g + 1) * POOL_GROUP)
        yp_s[:, cols] = _dot(act_bf[:, cols], poolw_ref[g])
    for hd in range(LRU_HEADS):
        cols = slice(hd * LRU_HEAD_DIM, (hd + 1) * LRU_HEAD_DIM)
        r = _dot(xs_bf[:, cols], wax_ref[hd])
        ra_s[:, cols] = r[:, :LRU_HEAD_DIM]
        rx_s[:, cols] = r[:, LRU_HEAD_DIM:]

    def pool_out(c, carry):
        r0 = pl.multiple_of(c * NORM_ROWS, NORM_ROWS)
        rows = pl.ds(r0, NORM_ROWS)
        y = yp_s[rows, :] * pscale_ref[...]
        mixed_s[rows, 0:W_POOL] = (y * _silu(gp_s[rows, :])).astype(BF16)
        return carry

    lax.fori_loop(0, M // NORM_ROWS, pool_out, 0)

    lam = lam_ref[...]
    neg = -lam
    softplus = jnp.maximum(neg, 0.0) + jnp.log1p(jnp.exp(-jnp.abs(neg)))
    c_half = (0.5 * LRU_C) * softplus
    steps = MIX_ROWS // B
    assert steps * B == MIX_ROWS

    for cb in range(W_LRU // LANES):
        cols = pl.ds(cb * LANES, LANES)
        ba = ba_ref[:, cols]
        bx = bx_ref[:, cols]
        ch = c_half[:, cb * LANES:(cb + 1) * LANES]

        def lru_chunk(c, h, cols=cols, ba=ba, bx=bx, ch=ch, cb=cb):
            r0 = pl.multiple_of(c * MIX_ROWS, MIX_ROWS)
            rows = pl.ds(r0, MIX_ROWS)
            t_r = jnp.tanh(0.5 * (ra_s[rows, cols] + ba))
            neg_log_a = ch * (1.0 + t_r)
            a = jnp.exp(-neg_log_a)
            one_m_a2 = jnp.tanh(neg_log_a) * (1.0 + a * a)
            m2 = jnp.maximum(one_m_a2, 1e-12)
            mult = m2 * lax.rsqrt(m2)
            if start_pos == 0:
                row = lax.broadcasted_iota(jnp.int32, (MIX_ROWS, LANES), 0) + (i * M + r0)
                mult = jnp.where(row < B, 1.0, mult)
            gi = 0.5 * (1.0 + jnp.tanh(0.5 * (rx_s[rows, cols] + bx)))
            u = xs_s[rows, cols] * gi * mult
            hs = []
            for s in range(steps):
                h = a[s * B:(s + 1) * B] * h + u[s * B:(s + 1) * B]
                hs.append(h)
            y = hs[0] if steps == 1 else jnp.concatenate(hs, axis=0)
            mixed_s[rows, pl.ds(W_POOL + cb * LANES, LANES)] = (y * _silu(gl_s[rows, cols])).astype(BF16)
            return h

        h_s[:, cols] = lax.fori_loop(0, M // MIX_ROWS, lru_chunk, h_s[:, cols])

    ra_s[...] = _dot(mixed_s[...], wout_ref[...])

    def post_norm(c, carry):
        r0 = pl.multiple_of(c * NORM_ROWS, NORM_ROWS)
        rows = pl.ds(r0, NORM_ROWS)
        o = ra_s[rows, :]
        ms = jnp.mean(o * o, axis=-1, keepdims=True)
        y_ref[rows, :] = x_ref[rows, :] + o * lax.rsqrt(ms + NORM_EPS) * gpost_ref[...]
        return carry

    lax.fori_loop(0, M // NORM_ROWS, post_norm, 0)

    new_pool = vext[M:M + PH, :]
    new_conv = uext[M:M + CH, :]

    @pl.when(i == n_t - 1)
    def _emit_state():
        npool_ref[...] = new_pool
        nconv_ref[...] = new_conv
        nh_ref[...] = h_s[...]

    @pl.when(i < n_t - 1)
    def _carry_state():
        vext[0:PH, :] = new_pool
        uext[0:CH, :] = new_conv


def _layer_call(x, pool0, conv0, h0, params, *, B, Tt, start_pos):
    NB, rows, _ = x.shape
    M = Tt * B
    n_t = rows // M
    assert n_t * M == rows and M % MIX_ROWS == 0 and MIX_ROWS % B == 0 and B % 8 == 0
    PH, CH = POOL_BUF * B, CONV_BUF * B

    def const_spec(a):
        nd = a.ndim
        return pl.BlockSpec(a.shape, lambda n, i, nd=nd: (0,) * nd, pipeline_mode=pl.Buffered(1))

    def tile_spec(width):
        return pl.BlockSpec((None, M, width), lambda n, i: (n, i, 0))

    def state_spec(r, width, **kw):
        return pl.BlockSpec((None, r, width), lambda n, i: (n, 0, 0), **kw)

    in_specs = [tile_spec(D_MODEL),
                state_spec(PH, W_POOL, pipeline_mode=pl.Buffered(1)),
                state_spec(CH, W_LRU, pipeline_mode=pl.Buffered(1)),
                state_spec(B, W_LRU, pipeline_mode=pl.Buffered(1))]
    in_specs += [const_spec(p) for p in params]
    out_specs = [tile_spec(D_MODEL), state_spec(PH, W_POOL), state_spec(CH, W_LRU), state_spec(B, W_LRU)]
    out_shape = [jax.ShapeDtypeStruct(x.shape, F32),
                 jax.ShapeDtypeStruct((NB, PH, W_POOL), F32),
                 jax.ShapeDtypeStruct((NB, CH, W_LRU), F32),
                 jax.ShapeDtypeStruct((NB, B, W_LRU), F32)]
    scratch = [
        pltpu.VMEM((PH + M, W_POOL), F32),
        pltpu.VMEM((CH + M, W_LRU), F32),
        pltpu.VMEM((B, W_LRU), F32),
        pltpu.VMEM((M, D_MODEL), BF16),
        pltpu.VMEM((M, W_LRU), BF16),
        pltpu.VMEM((M, W_POOL), F32),
        pltpu.VMEM((M, W_LRU), F32),
        pltpu.VMEM((M, W_LRU), F32),
        pltpu.VMEM((M, W_LRU), F32),
        pltpu.VMEM((M, W_LRU), F32),
        pltpu.VMEM((M, W_POOL), F32),
        pltpu.VMEM((M, D_MIX), BF16),
    ]

    def nbytes(shape, dtype):
        n = 1
        for s in shape:
            n *= s
        return n * jnp.dtype(dtype).itemsize

    scratch_bytes = sum(nbytes(s.shape, s.dtype) for s in scratch)
    const_bytes = sum(nbytes(p.shape, p.dtype) for p in params)
    state_bytes = (PH + CH + B) * W_POOL * 4
    io_bytes = 2 * 2 * M * D_MODEL * 4 + state_bytes + 2 * state_bytes
    staging_bytes = 2 * M * W_POOL * 4
    vmem_limit = min(V7X_VMEM_BYTES, scratch_bytes + const_bytes + io_bytes + staging_bytes + (4 << 20))

    body = functools.partial(_layer_body, B=B, Tt=Tt, start_pos=start_pos)
    return pl.pallas_call(
        body,
        grid=(NB, n_t),
        in_specs=in_specs,
        out_specs=out_specs,
        out_shape=out_shape,
        scratch_shapes=scratch,
        compiler_params=pltpu.CompilerParams(
            dimension_semantics=("arbitrary", "arbitrary"),
            vmem_limit_bytes=int(vmem_limit)),
        name=f"layer_b{B}_t{Tt}",
    )(x, pool0, conv0, h0, *params)


def _to_time_major(a, B):
    Bt, T, W = a.shape
    return a.reshape(Bt // B, B, T, W).transpose(0, 2, 1, 3).reshape(Bt // B, T * B, W)


def _from_time_major(a, B):
    NB, rows, W = a.shape
    T = rows // B
    return a.reshape(NB, T, B, W).transpose(0, 2, 1, 3).reshape(NB * B, T, W)


def kernel(x_prompt, x_sample, state_pool, state_conv, state_lru, norm_pre, norm_post, w_in, pool_w, pool_scale, conv_w, conv_b, lru_wa, lru_ba, lru_wx, lru_bx, lru_lam, w_out):
    PB, PT = 8, 64
    SB, ST = 64, 8
    n_prompt = x_prompt.shape[0]
    assert n_prompt == PB and x_sample.shape[0] % SB == 0 and x_sample.shape[1] == ST

    xp = _to_time_major(x_prompt, PB)
    xq = _to_time_major(x_sample, SB)
    zp = jnp.zeros((1, POOL_BUF * PB, W_POOL), F32)
    zc = jnp.zeros((1, CONV_BUF * PB, W_LRU), F32)
    zh = jnp.zeros((1, PB, W_LRU), F32)

    row = lambda v: v.reshape(1, -1)
    outs = [[] for _ in range(6)]
    for l in range(DEPTH):
        wax = jnp.concatenate([lru_wa[l], lru_wx[l]], axis=-1).astype(BF16)
        params = (row(norm_pre[l]), row(norm_post[l]), w_in[l].astype(BF16), pool_w[l].astype(BF16),
                  row(pool_scale[l]), conv_w[l], row(conv_b[l]), wax, row(lru_ba[l]), row(lru_bx[l]),
                  row(lru_lam[l]), w_out[l].astype(BF16))
        xp, pool_p, conv_p, h_p = _layer_call(xp, zp, zc, zh, params, B=PB, Tt=PT, start_pos=0)
        sp0 = _to_time_major(state_pool[l], SB)
        sc0 = _to_time_major(state_conv[l], SB)
        sh0 = state_lru[l].reshape(-1, SB, W_LRU)
        xq, pool_s, conv_s, h_s = _layer_call(xq, sp0, sc0, sh0, params, B=SB, Tt=ST, start_pos=PAST_LEN)
        for lst, val in zip(outs, (_from_time_major(pool_p, PB), _from_time_major(conv_p, PB),
                                   h_p.reshape(-1, W_LRU), _from_time_major(pool_s, SB),
                                   _from_time_major(conv_s, SB), h_s.reshape(-1, W_LRU))):
            lst.append(val)

    y_prompt = _from_time_major(xp, PB)
    y_sample = _from_time_major(xq, SB)
    return (y_prompt, y_sample) + tuple(jnp.stack(o, axis=0) for o in outs)
```

```python
import functools
import math

import jax
import jax.numpy as jnp
from jax import lax
from jax.experimental import pallas as pl
from jax.experimental.pallas import tpu as pltpu

D_MODEL = 1024
DEPTH = 4
W_POOL = 1024
W_LRU = 1024
D_MIX = W_POOL + W_LRU
D_IN = 2 * W_POOL + 2 * W_LRU
POOL_WINDOWS = (2, 4, 8, 16)
POOL_GROUP = W_POOL // len(POOL_WINDOWS)
POOL_BUF = 15
LRU_HEADS = 8
LRU_HEAD_DIM = W_LRU // LRU_HEADS
CONV_WIDTH = 4
CONV_BUF = CONV_WIDTH - 1
LRU_C = 8.0
NORM_EPS = 1e-6
PAST_LEN = 16384

LANES = 128
NORM_ROWS = 32
MIX_ROWS = 64
V7X_VMEM_BYTES = 64 * 1024 * 1024

F32 = jnp.float32
BF16 = jnp.bfloat16


def _dot(a, b):
    return jnp.dot(a, b, preferred_element_type=F32)


def _silu(g):
    h = 0.5 * g
    return h + h * jnp.tanh(h)


def _layer_body(x_ref, pool0_ref, conv0_ref, h0_ref, gpre_ref, gpost_ref, win_ref, poolw_ref,
                pscale_ref, cw_ref, cb_ref, wax_ref, ba_ref, bx_ref, lam_ref, wout_ref,
                y_ref, npool_ref, nconv_ref, nh_ref,
                vext, uext, h_s, xn_bf, d_bf, xs_bf, gp_s, gl_s, xs_s, ra_s, rx_s, yp_s, mixed_s,
                *, B, Tt, start_pos):
    i = pl.program_id(1)
    n_t = pl.num_programs(1)
    M = Tt * B
    PH = POOL_BUF * B
    CH = CONV_BUF * B

    @pl.when(i == 0)
    def _load_state():
        vext[0:PH, :] = pool0_ref[...]
        uext[0:CH, :] = conv0_ref[...]
        h_s[...] = h0_ref[...]

    for c in range(M // NORM_ROWS):
        rows = pl.ds(c * NORM_ROWS, NORM_ROWS)
        x = x_ref[rows, :]
        ms = jnp.mean(x * x, axis=-1, keepdims=True)
        xn = x * lax.rsqrt(ms + NORM_EPS) * gpre_ref[...]
        xn_bf[rows, :] = xn.astype(BF16)

    uext[CH:CH + M, :] = _dot(xn_bf[...], win_ref[:, 2 * W_POOL:2 * W_POOL + W_LRU])
    vext[PH:PH + M, :] = _dot(xn_bf[...], win_ref[:, 0:W_POOL])

    half_cw = 0.5 * cw_ref[...]
    half_cb = 0.5 * cb_ref[...]
    for c in range(M // MIX_ROWS):
        r0 = c * MIX_ROWS
        for cc in range(W_LRU // POOL_GROUP):
            cols = pl.ds(cc * POOL_GROUP, POOL_GROUP)
            lanes = slice(cc * POOL_GROUP, (cc + 1) * POOL_GROUP)
            acc = half_cb[:, lanes]
            for k in range(CONV_WIDTH):
                acc = acc + uext[pl.ds(k * B + r0, MIX_ROWS), cols] * half_cw[k:k + 1, lanes]
            xs_s[pl.ds(r0, MIX_ROWS), cols] = acc
            xs_bf[pl.ds(r0, MIX_ROWS), cols] = acc.astype(BF16)

    for hd in range(LRU_HEADS):
        cols = slice(hd * LRU_HEAD_DIM, (hd + 1) * LRU_HEAD_DIM)
        r = _dot(xs_bf[:, cols], wax_ref[hd])
        ra_s[:, cols] = r[:, :LRU_HEAD_DIM]
        rx_s[:, cols] = r[:, LRU_HEAD_DIM:]

    for g, w in enumerate(POOL_WINDOWS):
        cols = pl.ds(g * POOL_GROUP, POOL_GROUP)
        s = vext[PH - (w - 1) * B:PH + M, cols]
        cur = s[(w - 1) * B:]
        span = 1
        while span < w:
            s = s[span * B:] + s[:-span * B]
            span *= 2
        n_fill = max(0, min(Tt, w - 1 - start_pos))
        parts = []
        for t in range(n_fill):
            inv = jnp.where(i == 0, F32(1.0 / (start_pos + t + 1)), F32(1.0 / w))
            parts.append(s[t * B:(t + 1) * B] * inv)
        parts.append(s[n_fill * B:] * (1.0 / w))
        mean = parts[0] if len(parts) == 1 else jnp.concatenate(parts, axis=0)
        d_bf[:, cols] = (mean - cur).astype(BF16)

    gl_s[...] = _dot(xn_bf[...], win_ref[:, 2 * W_POOL + W_LRU:D_IN])

    lam = lam_ref[...]
    neg = -lam
    softplus = jnp.maximum(neg, 0.0) + jnp.log1p(jnp.exp(-jnp.abs(neg)))
    c_half = (0.5 * LRU_C) * softplus
    c_exp2 = (-math.log2(math.e)) * c_half
    half_ba = 0.5 * ba_ref[...]
    half_bx = 0.5 * bx_ref[...]
    steps = MIX_ROWS // B
    assert steps * B == MIX_ROWS

    for cb in range(W_LRU // LANES):
        cols = pl.ds(cb * LANES, LANES)
        lanes = slice(cb * LANES, (cb + 1) * LANES)
        h = h_s[:, cols]
        for c in range(M // MIX_ROWS):
            rows = pl.ds(c * MIX_ROWS, MIX_ROWS)
            wr = 1.0 + jnp.tanh(ra_s[rows, cols] + half_ba[:, lanes])
            neg_log_a = c_half[:, lanes] * wr
            a = jnp.exp2(c_exp2[:, lanes] * wr)
            one_m_a2 = jnp.tanh(neg_log_a) * (1.0 + a * a)
            m2 = jnp.maximum(one_m_a2, 1e-12)
            mult = m2 * lax.rsqrt(m2)
            if start_pos == 0 and c == 0:
                first = (lax.broadcasted_iota(jnp.int32, (MIX_ROWS, LANES), 0) < B) & (i == 0)
                mult = jnp.where(first, 1.0, mult)
            wi = 1.0 + jnp.tanh(rx_s[rows, cols] + half_bx[:, lanes])
            u = xs_s[rows, cols] * wi * mult
            hs = []
            for s in range(steps):
                h = a[s * B:(s + 1) * B] * h + u[s * B:(s + 1) * B]
                hs.append(h)
            y = hs[0] if steps == 1 else jnp.concatenate(hs, axis=0)
            mixed_s[rows, pl.ds(W_POOL + cb * LANES, LANES)] = (y * _silu(gl_s[rows, cols])).astype(BF16)
        h_s[:, cols] = h

    for g in range(len(POOL_WINDOWS)):
        cols = slice(g * POOL_GROUP, (g + 1) * POOL_GROUP)
        yp_s[:, cols] = _dot(d_bf[:, cols], poolw_ref[g])
    gp_s[...] = _dot(xn_bf[...], win_ref[:, W_POOL:2 * W_POOL])

    ra_s[...] = _dot(mixed_s[:, W_POOL:], wout_ref[W_POOL:, :])

    for c in range(M // NORM_ROWS):
        rows = pl.ds(c * NORM_ROWS, NORM_ROWS)
        y = yp_s[rows, :] * pscale_ref[...]
        mixed_s[rows, 0:W_POOL] = (y * _silu(gp_s[rows, :])).astype(BF16)

    rx_s[...] = _dot(mixed_s[:, 0:W_POOL], wout_ref[0:W_POOL, :])

    for c in range(M // NORM_ROWS):
        rows = pl.ds(c * NORM_ROWS, NORM_ROWS)
        o = rx_s[rows, :] + ra_s[rows, :]
        ms = jnp.mean(o * o, axis=-1, keepdims=True)
        y_ref[rows, :] = x_ref[rows, :] + o * lax.rsqrt(ms + NORM_EPS) * gpost_ref[...]

    new_pool = vext[M:M + PH, :]
    new_conv = uext[M:M + CH, :]

    @pl.when(i == n_t - 1)
    def _emit_state():
        npool_ref[...] = new_pool
        nconv_ref[...] = new_conv
        nh_ref[...] = h_s[...]

    @pl.when(i < n_t - 1)
    def _carry_state():
        vext[0:PH, :] = new_pool
        uext[0:CH, :] = new_conv


def _layer_call(x, pool0, conv0, h0, params, *, B, Tt, start_pos):
    NB, rows, _ = x.shape
    M = Tt * B
    n_t = rows // M
    assert n_t * M == rows and M % MIX_ROWS == 0 and MIX_ROWS % B == 0 and B % 8 == 0
    PH, CH = POOL_BUF * B, CONV_BUF * B

    def const_spec(a):
        nd = a.ndim
        return pl.BlockSpec(a.shape, lambda n, i, nd=nd: (0,) * nd, pipeline_mode=pl.Buffered(1))

    def tile_spec(width):
        return pl.BlockSpec((None, M, width), lambda n, i: (n, i, 0))

    def state_spec(r, width, **kw):
        return pl.BlockSpec((None, r, width), lambda n, i: (n, 0, 0), **kw)

    in_specs = [tile_spec(D_MODEL),
                state_spec(PH, W_POOL, pipeline_mode=pl.Buffered(1)),
                state_spec(CH, W_LRU, pipeline_mode=pl.Buffered(1)),
                state_spec(B, W_LRU, pipeline_mode=pl.Buffered(1))]
    in_specs += [const_spec(p) for p in params]
    out_specs = [tile_spec(D_MODEL), state_spec(PH, W_POOL), state_spec(CH, W_LRU), state_spec(B, W_LRU)]
    out_shape = [jax.ShapeDtypeStruct(x.shape, F32),
                 jax.ShapeDtypeStruct((NB, PH, W_POOL), F32),
                 jax.ShapeDtypeStruct((NB, CH, W_LRU), F32),
                 jax.ShapeDtypeStruct((NB, B, W_LRU), F32)]
    scratch = [
        pltpu.VMEM((PH + M, W_POOL), F32),
        pltpu.VMEM((CH + M, W_LRU), F32),
        pltpu.VMEM((B, W_LRU), F32),
        pltpu.VMEM((M, D_MODEL), BF16),
        pltpu.VMEM((M, W_POOL), BF16),
        pltpu.VMEM((M, W_LRU), BF16),
        pltpu.VMEM((M, W_POOL), F32),
        pltpu.VMEM((M, W_LRU), F32),
        pltpu.VMEM((M, W_LRU), F32),
        pltpu.VMEM((M, W_LRU), F32),
        pltpu.VMEM((M, W_LRU), F32),
        pltpu.VMEM((M, W_POOL), F32),
        pltpu.VMEM((M, D_MIX), BF16),
    ]

    def nbytes(shape, dtype):
        n = 1
        for s in shape:
            n *= s
        return n * jnp.dtype(dtype).itemsize

    scratch_bytes = sum(nbytes(s.shape, s.dtype) for s in scratch)
    const_bytes = sum(nbytes(p.shape, p.dtype) for p in params)
    state_bytes = (PH + CH + B) * W_POOL * 4
    io_bytes = 2 * 2 * M * D_MODEL * 4 + state_bytes + 2 * state_bytes
    staging_bytes = 2 * M * W_POOL * 4
    vmem_limit = min(V7X_VMEM_BYTES, scratch_bytes + const_bytes + io_bytes + staging_bytes + (4 << 20))

    body = functools.partial(_layer_body, B=B, Tt=Tt, start_pos=start_pos)
    return pl.pallas_call(
        body,
        grid=(NB, n_t),
        in_specs=in_specs,
        out_specs=out_specs,
        out_shape=out_shape,
        scratch_shapes=scratch,
        compiler_params=pltpu.CompilerParams(
            dimension_semantics=("arbitrary", "arbitrary"),
            vmem_limit_bytes=int(vmem_limit)),
        name=f"layer_b{B}_t{Tt}",
    )(x, pool0, conv0, h0, *params)


def _to_time_major(a, B):
    Bt, T, W = a.shape
    return a.reshape(Bt // B, B, T, W).transpose(0, 2, 1, 3).reshape(Bt // B, T * B, W)


def _from_time_major(a, B):
    NB, rows, W = a.shape
    T = rows // B
    return a.reshape(NB, T, B, W).transpose(0, 2, 1, 3).reshape(NB * B, T, W)


def kernel(x_prompt, x_sample, state_pool, state_conv, state_lru, norm_pre, norm_post, w_in, pool_w, pool_scale, conv_w, conv_b, lru_wa, lru_ba, lru_wx, lru_bx, lru_lam, w_out):
    PB, PT = 8, 64
    SB, ST = 64, 8
    n_prompt = x_prompt.shape[0]
    assert n_prompt == PB and x_sample.shape[0] % SB == 0 and x_sample.shape[1] == ST

    xp = _to_time_major(x_prompt, PB)
    xq = _to_time_major(x_sample, SB)
    zp = jnp.zeros((1, POOL_BUF * PB, W_POOL), F32)
    zc = jnp.zeros((1, CONV_BUF * PB, W_LRU), F32)
    zh = jnp.zeros((1, PB, W_LRU), F32)

    row = lambda v: v.reshape(1, -1)
    outs = [[] for _ in range(6)]
    for l in range(DEPTH):
        wax = jnp.concatenate([lru_wa[l], lru_wx[l]], axis=-1).astype(BF16)
        params = (row(norm_pre[l]), row(norm_post[l]), w_in[l].astype(BF16), pool_w[l].astype(BF16),
                  row(pool_scale[l]), conv_w[l], row(conv_b[l]), wax, row(lru_ba[l]), row(lru_bx[l]),
                  row(lru_lam[l]), w_out[l].astype(BF16))
        xp, pool_p, conv_p, h_p = _layer_call(xp, zp, zc, zh, params, B=PB, Tt=PT, start_pos=0)
        sp0 = _to_time_major(state_pool[l], SB)
        sc0 = _to_time_major(state_conv[l], SB)
        sh0 = state_lru[l].reshape(-1, SB, W_LRU)
        xq, pool_s, conv_s, h_s = _layer_call(xq, sp0, sc0, sh0, params, B=SB, Tt=ST, start_pos=PAST_LEN)
        for lst, val in zip(outs, (_from_time_major(pool_p, PB), _from_time_major(conv_p, PB),
                                   h_p.reshape(-1, W_LRU), _from_time_major(pool_s, SB),
                                   _from_time_major(conv_s, SB), h_s.reshape(-1, W_LRU))):
            lst.append(val)

    y_prompt = _from_time_major(xp, PB)
    y_sample = _from_time_major(xq, SB)
    return (y_prompt, y_sample) + tuple(jnp.stack(o, axis=0) for o in outs)
```
